```python
import math
import jax
import jax.numpy as jnp
from jax import lax
import numpy as np

D_MODEL = 1024
BATCH = 4
SEQ = 4096
DEPTH = 2

CTX_LEN = 256
GRID_W = 64
EPS = 1e-6
CHUNK = 16
F32 = jnp.float32

HG_HEADS = 4
HG_DK = 128
HG_DV = 128
HG_KW = HG_HEADS * HG_DK
HG_VW = HG_HEADS * HG_DV
GLA_HEADS = 4
GLA_DK = 64
GLA_DV = 128
GLA_KW = GLA_HEADS * GLA_DK
GLA_VW = GLA_HEADS * GLA_DV
GLA_RANK = 16
GLA_GATE_NORM = 16.0
S5_GROUP = 16
S5_GROUPS = 32
S5_WIDTH = S5_GROUP * S5_GROUPS
S5_STATE = 64
N_BRANCH = 3
BRANCH_W = 512
FFN_DIM = 3584
N_EXPERTS = 8
TOP_K = 2
N_DENSE = (DEPTH + 1) // 2
N_MOE = DEPTH // 2

IN_SPLITS = (HG_KW, HG_KW, HG_KW, HG_VW, HG_VW,
             GLA_KW, GLA_KW, GLA_VW, GLA_RANK, GLA_RANK, GLA_VW,
             S5_WIDTH, D_MODEL, D_MODEL, D_MODEL)
N_IN = 3 * HG_KW + 2 * HG_VW + 2 * GLA_KW + 2 * GLA_VW + 2 * GLA_RANK + S5_WIDTH + N_BRANCH * D_MODEL

kernel_name = 'hybrid_hgrn2_gla_s5_moe_dit'


def _rmsnorm(x, g):
    xf = x.astype(F32)
    y = xf * lax.rsqrt(jnp.mean(xf * xf, axis=-1, keepdims=True) + EPS)
    return (y * g.astype(F32)).astype(x.dtype)


def _modulate(xn, shift, scale):
    return xn * (1 + scale[:, None, :]) + shift[:, None, :]


def _grid_pos_embed(rows, dim, dtype):
    rr, cc = jnp.meshgrid(jnp.arange(rows, dtype=F32), jnp.arange(GRID_W, dtype=F32), indexing='ij')
    quarter = dim // 4
    omega = 1.0 / (10000.0 ** (jnp.arange(quarter, dtype=F32) / quarter))

    def emb(p):
        ang = p.reshape(-1)[:, None] * omega[None, :]
        return jnp.concatenate([jnp.sin(ang), jnp.cos(ang)], axis=-1)

    return jnp.concatenate([emb(rr), emb(cc)], axis=-1).astype(dtype)


def _split_cols(p):
    out = []
    start = 0
    for width in IN_SPLITS:
        out.append(p[..., start:start + width])
        start += width
    return out


def _heads(a, n_heads):
    bsz, t, w = a.shape
    return a.astype(F32).reshape(bsz, t, n_heads, w // n_heads).transpose(0, 2, 1, 3)


def _head_norm_gate(o, gain, gate):
    o = jnp.swapaxes(o, 1, 2)
    bsz, t, h, dv = o.shape
    o = o * lax.rsqrt(jnp.mean(o * o, axis=-1, keepdims=True) + EPS) * gain.astype(F32).reshape(h, dv)
    return o.reshape(bsz, t, h * dv) * jax.nn.silu(gate.astype(F32))


def _chunk_gated_linear(q, k, v, log_g, s0):
    bsz, h, l, dk = q.shape
    dv = v.shape[-1]
    n = l // CHUNK

    def chunks(a):
        return jnp.moveaxis(a.reshape(bsz, h, n, CHUNK, a.shape[-1]), 2, 0)

    lower = jnp.tril(jnp.ones((CHUNK, CHUNK), dtype=bool))[:, :, None]

    def step(s, inp):
        qc, kc, vc, gc = inp
        bc = jnp.cumsum(gc, axis=2)
        last = bc[:, :, -1:, :]
        rel = bc[:, :, :, None, :] - bc[:, :, None, :, :]
        decay = jnp.exp(jnp.where(lower, rel, -jnp.inf))
        scores = jnp.einsum('bhtd,bhsd,bhtsd->bhts', qc, kc, decay)
        o = (jnp.einsum('bhts,bhsv->bhtv', scores, vc)
             + jnp.einsum('bhtd,bhdv->bhtv', qc * jnp.exp(bc), s))
        s_new = (jnp.exp(last[:, :, 0, :])[..., None] * s
                 + jnp.einsum('bhsd,bhsv->bhdv', kc * jnp.exp(last - bc), vc))
        return s_new, o

    s_fin, o = lax.scan(step, s0, (chunks(q), chunks(k), chunks(v), chunks(log_g)))
    return jnp.moveaxis(o, 0, 2).reshape(bsz, h, l, dv), s_fin


def _bidir_prefix(q, k_f, k_b, v, g_f, g_b, n_ctx):
    bsz, h, _, dk = q.shape
    dv = v.shape[-1]
    zero = jnp.zeros((bsz, h, dk, dv), F32)

    def cx(a):
        return a[:, :, :n_ctx]

    def lt(a):
        return a[:, :, n_ctx:]

    def fl(a):
        return jnp.flip(a, axis=2)

    oc_f, sc_f = _chunk_gated_linear(cx(q), cx(k_f), cx(v), cx(g_f), zero)
    oc_b, sc_b = _chunk_gated_linear(fl(cx(q)), fl(cx(k_b)), fl(cx(v)), fl(cx(g_b)), zero)
    ol_f, _ = _chunk_gated_linear(lt(q), lt(k_f), lt(v), lt(g_f), sc_f)
    ol_b, _ = _chunk_gated_linear(fl(lt(q)), fl(lt(k_b)), fl(lt(v)), fl(lt(g_b)), sc_b)
    return jnp.concatenate([oc_f + fl(oc_b), ol_f + fl(ol_b)], axis=2)


def _hgrn2_branch(pq, pff, pfb, pi, pg, lb, norm_g, n_ctx):
    lb = lb.astype(F32)
    log_lb = jnp.log(lb)
    log_ub = jnp.log1p(-lb)

    def forget(z):
        z = z.astype(F32)
        log_f = jnp.logaddexp(log_lb, log_ub + jax.nn.log_sigmoid(z))
        one_minus_f = (1.0 - lb) * jax.nn.sigmoid(-z)
        return _heads(log_f, HG_HEADS), _heads(one_minus_f, HG_HEADS)

    g_f, k_f = forget(pff)
    g_b, k_b = forget(pfb)
    o = _bidir_prefix(_heads(pq, HG_HEADS), k_f, k_b, _heads(pi, HG_HEADS), g_f, g_b, n_ctx)
    return _head_norm_gate(o, norm_g, pg)


def _gla_branch(pq, pk, pv, pdf, pdb, pr, gate_up, gate_b, norm_g, n_ctx):
    up = gate_up.astype(F32)
    ub = gate_b.astype(F32)
    g_f = jax.nn.log_sigmoid(pdf.astype(F32) @ up[0] + ub[0]) / GLA_GATE_NORM
    g_b = jax.nn.log_sigmoid(pdb.astype(F32) @ up[1] + ub[1]) / GLA_GATE_NORM
    q = _heads(pq, GLA_HEADS) * (GLA_DK ** -0.5)
    k = _heads(pk, GLA_HEADS)
    o = _bidir_prefix(q, k, k, _heads(pv, GLA_HEADS), _heads(g_f, GLA_HEADS), _heads(g_b, GLA_HEADS), n_ctx)
    return _head_norm_gate(o, norm_g, pr)


def _complex_affine_combine(e1, e2):
    a1r, a1i, b1r, b1i = e1
    a2r, a2i, b2r, b2i = e2
    return (a2r * a1r - a2i * a1i,
            a2r * a1i + a2i * a1r,
            a2r * b1r - a2i * b1i + b2r,
            a2r * b1i + a2i * b1r + b2i)


def _s5_scan(abar_re, abar_im, bu_re, bu_im, s0_re, s0_im, reverse):
    if reverse:
        bu_re = jnp.flip(bu_re, axis=1)
        bu_im = jnp.flip(bu_im, axis=1)
    bu_re = bu_re.at[:, 0].add(abar_re * s0_re - abar_im * s0_im)
    bu_im = bu_im.at[:, 0].add(abar_re * s0_im + abar_im * s0_re)
    a_re = jnp.broadcast_to(abar_re, bu_re.shape)
    a_im = jnp.broadcast_to(abar_im, bu_im.shape)
    _, _, s_re, s_im = lax.associative_scan(_complex_affine_combine, (a_re, a_im, bu_re, bu_im), axis=1)
    fin_re, fin_im = s_re[:, -1], s_im[:, -1]
    if reverse:
        s_re = jnp.flip(s_re, axis=1)
        s_im = jnp.flip(s_im, axis=1)
    return s_re, s_im, fin_re, fin_im


def _s5_branch(u, n_ctx, a_re, a_im, log_step, b_re, b_im, c_re, c_im, d_skip, glu_w):
    uf = u.astype(F32)
    bsz, t, _ = uf.shape
    ug = uf.reshape(bsz, t, S5_GROUPS, S5_GROUP)
    y = uf * d_skip.astype(F32)
    zero = jnp.zeros((bsz, S5_GROUPS, S5_STATE), F32)
    for direction in range(2):
        lam_re = a_re[direction].astype(F32)
        lam_im = a_im[direction].astype(F32)
        step = jnp.exp(log_step[direction].astype(F32))[:, None]
        mag = jnp.exp(lam_re * step)
        abar_re = mag * jnp.cos(lam_im * step)
        abar_im = mag * jnp.sin(lam_im * step)
        den = lam_re * lam_re + lam_im * lam_im
        z_re = ((abar_re - 1.0) * lam_re + abar_im * lam_im) / den
        z_im = (abar_im * lam_re - (abar_re - 1.0) * lam_im) / den
        br = b_re[direction].astype(F32)
        bi = b_im[direction].astype(F32)
        bbar_re = z_re[..., None] * br - z_im[..., None] * bi
        bbar_im = z_re[..., None] * bi + z_im[..., None] * br
        bu_re = jnp.einsum('btgp,gnp->btgn', ug, bbar_re)
        bu_im = jnp.einsum('btgp,gnp->btgn', ug, bbar_im)
        rev = direction == 1
        sc_re, sc_im, f_re, f_im = _s5_scan(abar_re, abar_im, bu_re[:, :n_ctx], bu_im[:, :n_ctx], zero, zero, rev)
        sl_re, sl_im, _, _ = _s5_scan(abar_re, abar_im, bu_re[:, n_ctx:], bu_im[:, n_ctx:], f_re, f_im, rev)
        s_re = jnp.concatenate([sc_re, sl_re], axis=1)
        s_im = jnp.concatenate([sc_im, sl_im], axis=1)
        cr = c_re[direction].astype(F32)
        ci = c_im[direction].astype(F32)
        y = y + (jnp.einsum('btgn,gpn->btgp', s_re, cr)
                 - jnp.einsum('btgn,gpn->btgp', s_im, ci)).reshape(bsz, t, S5_WIDTH)
    y = jax.nn.gelu(y)
    return y * jax.nn.sigmoid(y @ glu_w.astype(F32))


def _mixer(h, n_ctx, w_in, lb, hg_norm_g, gla_gate_up, gla_gate_b, gla_norm_g, s5_a_re, s5_a_im,
           s5_log_step, s5_b_re, s5_b_im, s5_c_re, s5_c_im, s5_d, s5_glu_w, branch_proj, w_out):
    (hg_q, hg_ff, hg_fb, hg_i, hg_g, gl_q, gl_k, gl_v, gl_df, gl_db, gl_r, s5_u,
     gate_a, gate_b, gate_c) = _split_cols(h @ w_in)
    y_a = _hgrn2_branch(hg_q, hg_ff, hg_fb, hg_i, hg_g, lb, hg_norm_g, n_ctx)
    y_b = _gla_branch(gl_q, gl_k, gl_v, gl_df, gl_db, gl_r, gla_gate_up, gla_gate_b, gla_norm_g, n_ctx)
    y_c = _s5_branch(s5_u, n_ctx, s5_a_re, s5_a_im, s5_log_step, s5_b_re, s5_b_im, s5_c_re, s5_c_im,
                     s5_d, s5_glu_w)
    merged = jnp.zeros(h.shape, F32)
    for idx, (y_k, gate_k) in enumerate(((y_a, gate_a), (y_b, gate_b), (y_c, gate_c))):
        merged = merged + jax.nn.sigmoid(gate_k.astype(F32)) * (y_k.astype(h.dtype) @ branch_proj[idx]).astype(F32)
    return merged.astype(h.dtype) @ w_out


def _swiglu(h, w1, w3, w2):
    return (jax.nn.silu(h @ w1) * (h @ w3)) @ w2


def _moe(h, router_w, w1, w3, w2):
    probs = jax.nn.softmax((h @ router_w).astype(F32), axis=-1)
    top_p, top_i = lax.top_k(probs, TOP_K)
    top_p = top_p / jnp.sum(top_p, axis=-1, keepdims=True)
    combine = jnp.sum(jax.nn.one_hot(top_i, N_EXPERTS, dtype=F32) * top_p[..., None], axis=-2)
    out = jnp.zeros(h.shape, F32)
    for e in range(N_EXPERTS):
        out = out + combine[..., e:e + 1] * _swiglu(h, w1[e], w3[e], w2[e]).astype(F32)
    return out.astype(h.dtype)


def setup_inputs(seed: int = 0) -> dict:
    key = jax.random.key(seed)
    keys = iter(jax.random.split(key, 48))

    def nrm(shape, scale):
        return scale * jax.random.normal(next(keys), shape, F32)

    def gain(shape):
        return 1.0 + nrm(shape, 0.01)

    d = D_MODEL
    state_idx = jnp.arange(S5_STATE, dtype=F32)
    s5_shape = (DEPTH, 2, S5_GROUPS, S5_STATE)
    return {
        'x': nrm((BATCH, SEQ, d), 1.0),
        'c': nrm((BATCH, d), 1.0),
        'ctx': nrm((BATCH, CTX_LEN, d), 1.0),
        'c_ctx': nrm((d,), 1.0),
        'norm1_g': gain((DEPTH, d)),
        'norm2_g': gain((DEPTH, d)),
        'ada_w': nrm((DEPTH, d, 6 * d), 0.5 * d ** -0.5),
        'ada_b': nrm((DEPTH, 6 * d), 0.01),
        'w_in': nrm((DEPTH, d, N_IN), d ** -0.5),
        'hg_lb_logits': nrm((DEPTH, HG_KW), 0.1),
        'hg_norm_g': gain((DEPTH, HG_VW)),
        'gla_gate_up': nrm((DEPTH, 2, GLA_RANK, GLA_KW), GLA_RANK ** -0.5),
        'gla_gate_b': nrm((DEPTH, 2, GLA_KW), 0.01),
        'gla_norm_g': gain((DEPTH, GLA_VW)),
        's5_a_re': -0.5 + nrm(s5_shape, 0.01),
        's5_a_im': math.pi * state_idx + nrm(s5_shape, 0.01),
        's5_log_step': jax.random.uniform(next(keys), (DEPTH, 2, S5_GROUPS), F32, math.log(1e-3), math.log(1e-1)),
        's5_b_re': nrm((DEPTH, 2, S5_GROUPS, S5_STATE, S5_GROUP), (2 * S5_GROUP) ** -0.5),
        's5_b_im': nrm((DEPTH, 2, S5_GROUPS, S5_STATE, S5_GROUP), (2 * S5_GROUP) ** -0.5),
        's5_c_re': nrm((DEPTH, 2, S5_GROUPS, S5_GROUP, S5_STATE), S5_STATE ** -0.5),
        's5_c_im': nrm((DEPTH, 2, S5_GROUPS, S5_GROUP, S5_STATE), S5_STATE ** -0.5),
        's5_d': nrm((DEPTH, S5_WIDTH), 1.0),
        's5_glu_w': nrm((DEPTH, S5_WIDTH, S5_WIDTH), S5_WIDTH ** -0.5),
        'branch_proj': nrm((DEPTH, N_BRANCH, BRANCH_W, d), BRANCH_W ** -0.5),
        'w_out': nrm((DEPTH, d, d), d ** -0.5),
        'ffn_w1': nrm((N_DENSE, d, FFN_DIM), d ** -0.5),
        'ffn_w3': nrm((N_DENSE, d, FFN_DIM), d ** -0.5),
        'ffn_w2': nrm((N_DENSE, FFN_DIM, d), FFN_DIM ** -0.5),
        'router_w': nrm((N_MOE, d, N_EXPERTS), d ** -0.5),
        'moe_w1': nrm((N_MOE, N_EXPERTS, d, FFN_DIM), d ** -0.5),
        'moe_w3': nrm((N_MOE, N_EXPERTS, d, FFN_DIM), d ** -0.5),
        'moe_w2': nrm((N_MOE, N_EXPERTS, FFN_DIM, d), FFN_DIM ** -0.5),
        'final_norm_g': gain((d,)),
    }


def reference(x, c, ctx, c_ctx, norm1_g, norm2_g, ada_w, ada_b, w_in, hg_lb_logits, hg_norm_g,
              gla_gate_up, gla_gate_b, gla_norm_g, s5_a_re, s5_a_im, s5_log_step, s5_b_re, s5_b_im,
              s5_c_re, s5_c_im, s5_d, s5_glu_w, branch_proj, w_out, ffn_w1, ffn_w3, ffn_w2,
              router_w, moe_w1, moe_w3, moe_w2, final_norm_g):
    n_tok = x.shape[1]
    n_ctx = ctx.shape[1]
    rows = n_tok // GRID_W
    x = x + _grid_pos_embed(rows, D_MODEL, x.dtype)[None]
    lb_cum = jnp.cumsum(jax.nn.softmax(hg_lb_logits.astype(F32), axis=0), axis=0)
    lower_bounds = lb_cum - lb_cum[0:1]
    cond = jax.nn.silu(c)
    cond_ctx = jax.nn.silu(c_ctx)[None]
    for layer in range(DEPTH):
        last = layer == DEPTH - 1
        mod = cond @ ada_w[layer] + ada_b[layer]
        mod_c = cond_ctx @ ada_w[layer] + ada_b[layer]
        sh1, sc1, g1, sh2, sc2, g2 = jnp.split(mod, 6, axis=-1)
        csh1, csc1, cg1, csh2, csc2, cg2 = jnp.split(mod_c, 6, axis=-1)
        h = jnp.concatenate([_modulate(_rmsnorm(ctx, norm1_g[layer]), csh1, csc1),
                             _modulate(_rmsnorm(x, norm1_g[layer]), sh1, sc1)], axis=1)
        y = _mixer(h, n_ctx, w_in[layer], lower_bounds[layer], hg_norm_g[layer], gla_gate_up[layer],
                   gla_gate_b[layer], gla_norm_g[layer], s5_a_re[layer], s5_a_im[layer], s5_log_step[layer],
                   s5_b_re[layer], s5_b_im[layer], s5_c_re[layer], s5_c_im[layer], s5_d[layer],
                   s5_glu_w[layer], branch_proj[layer], w_out[layer])
        x = x + g1[:, None, :] * y[:, n_ctx:]
        if not last:
            ctx = ctx + cg1[:, None, :] * y[:, n_ctx * 0:n_ctx]
        h_lat = _modulate(_rmsnorm(x, norm2_g[layer]), sh2, sc2)
        if last:
            h = h_lat
        else:
            h = jnp.concatenate([_modulate(_rmsnorm(ctx, norm2_g[layer]), csh2, csc2), h_lat], axis=1)
        if layer % 2 == 0:
            f = _swiglu(h, ffn_w1[layer // 2], ffn_w3[layer // 2], ffn_w2[layer // 2])
        else:
            f = _moe(h, router_w[layer // 2], moe_w1[layer // 2], moe_w3[layer // 2], moe_w2[layer // 2])
        if last:
            x = x + g2[:, None, :] * f
        else:
            x = x + g2[:, None, :] * f[:, n_ctx:]
            ctx = ctx + cg2[:, None, :] * f[:, :n_ctx]
    return _rmsnorm(x, final_norm_g)
```

```python
import functools
import math

import jax
import jax.numpy as jnp
from jax import lax
from jax.experimental import pallas as pl
from jax.experimental.pallas import tpu as pltpu

F32 = jnp.float32
BF16 = jnp.bfloat16
HIGHEST = lax.Precision.HIGHEST
EPS = 1e-6

D_MODEL = 1024
GRID_W = 64
HG_HEADS = 4
HG_KW = 512
GLA_HEADS = 4
GLA_DK = 64
GLA_KW = 256
GLA_RANK = 16
GLA_GATE_NORM = 16.0
S5_GROUP = 16
S5_GROUPS = 32
S5_STATE = 64
S5_WIDTH = 512
N_EXPERTS = 8

LANES = 128
SUBLANES = 8
CHUNK = 128
BLOCK = 16
NBLK = CHUNK // BLOCK
VMEM_LIMIT = 56 * 1024 * 1024

P_HG_Q, P_HG_FF, P_HG_FB, P_HG_I, P_HG_G = 0, 512, 1024, 1536, 2048
P_GL_Q, P_GL_K, P_GL_V = 2560, 2816, 3072
P_GL_R = 3584
P_GATES = 4096
P_S5_U = 7168
P_GL_D = 7680
P_WIDTH = 7936
P_TN = P_WIDTH // 2


def _row_tile(n_rows, cap):
    best = SUBLANES
    for t in range(SUBLANES, cap + 1, SUBLANES):
        if n_rows % t == 0:
            best = t
    return best


def _sigmoid(x):
    return 1.0 / (1.0 + jnp.exp(-x))


def _log_sigmoid(x):
    return jnp.minimum(x, 0.0) - jnp.log1p(jnp.exp(-jnp.abs(x)))


def _silu(x):
    return x * _sigmoid(x)


def _gelu_tanh(x):
    return 0.5 * x * (1.0 + jnp.tanh(math.sqrt(2.0 / math.pi) * (x + 0.044715 * (x * x * x))))


def _norm_mod(x, gain, mod, row0, n_ctx):
    ms = jnp.mean(x * x, axis=-1, keepdims=True)
    y = x * lax.rsqrt(ms + EPS) * gain
    if n_ctx > 0:
        row = row0 + lax.broadcasted_iota(jnp.int32, (x.shape[0], 1), 0)
        is_ctx = row < n_ctx
        shift = jnp.where(is_ctx, mod[3:4], mod[0:1])
        scale = jnp.where(is_ctx, mod[4:5], mod[1:2])
    else:
        shift, scale = mod[0:1], mod[1:2]
    return y * (1.0 + scale) + shift


def _res_gate(mod, n_rows, row0, n_ctx):
    if n_ctx > 0:
        row = row0 + lax.broadcasted_iota(jnp.int32, (n_rows, 1), 0)
        return jnp.where(row < n_ctx, mod[5:6], mod[2:3])
    return mod[2:3]


def _mod_kernel(c_ref, w_ref, b_ref, o_ref):
    c = c_ref[...]
    o_ref[0] = jnp.dot(_silu(c), w_ref[0], preferred_element_type=F32, precision=HIGHEST) + b_ref[0]


def _modulation(cvec, ada_w, ada_b):
    depth, d, n = ada_w.shape
    tn = 1536
    return pl.pallas_call(
        _mod_kernel,
        out_shape=jax.ShapeDtypeStruct((depth, SUBLANES, n), F32),
        grid=(depth, n // tn),
        in_specs=[
            pl.BlockSpec((SUBLANES, d), lambda l, j: (0, 0)),
            pl.BlockSpec((1, d, tn), lambda l, j: (l, 0, j)),
            pl.BlockSpec((1, 1, tn), lambda l, j: (l, 0, j)),
        ],
        out_specs=pl.BlockSpec((1, SUBLANES, tn), lambda l, j: (l, 0, j)),
        compiler_params=pltpu.CompilerParams(
            dimension_semantics=("parallel", "parallel"), vmem_limit_bytes=VMEM_LIMIT),
        name="adaln_mod",
    )(cvec, ada_w, ada_b.reshape(depth, 1, n))


def _inproj_kernel(x_ref, g_ref, mod_ref, w_ref, o_ref, *, n_ctx, tm):
    i = pl.program_id(2)
    h = _norm_mod(x_ref[0], g_ref[...], mod_ref[0], i * tm, n_ctx)
    o_ref[0] = jnp.dot(h.astype(BF16), w_ref[...], preferred_element_type=F32)


def _in_projection(xc, gain, mod, w_p, n_ctx):
    bsz, t_all, d = xc.shape
    tm = _row_tile(t_all, 544)
    return pl.pallas_call(
        functools.partial(_inproj_kernel, n_ctx=n_ctx, tm=tm),
        out_shape=jax.ShapeDtypeStruct((bsz, t_all, P_WIDTH), F32),
        grid=(P_WIDTH // P_TN, bsz, t_all // tm),
        in_specs=[
            pl.BlockSpec((1, tm, d), lambda j, b, i: (b, i, 0)),
            pl.BlockSpec((1, d), lambda j, b, i: (0, 0)),
            pl.BlockSpec((1, 6, d), lambda j, b, i: (b, 0, 0)),
            pl.BlockSpec((d, P_TN), lambda j, b, i: (0, j)),
        ],
        out_specs=pl.BlockSpec((1, tm, P_TN), lambda j, b, i: (b, i, j)),
        compiler_params=pltpu.CompilerParams(
            dimension_semantics=("parallel", "parallel", "parallel"), vmem_limit_bytes=VMEM_LIMIT),
        name="in_projection",
    )(xc, gain.reshape(1, d), mod, w_p)


def _chunk_scan(q, k, g, v, st, reverse):
    row = lax.broadcasted_iota(jnp.int32, (CHUNK, CHUNK), 0)
    col = lax.broadcasted_iota(jnp.int32, (CHUNK, CHUNK), 1)
    tri = jnp.where((row <= col) if reverse else (row >= col), 1.0, 0.0).astype(F32)
    b = jnp.dot(tri, g, preferred_element_type=F32, precision=HIGHEST)
    b_edge = b[0:1] if reverse else b[CHUNK - 1:CHUNK]

    qs = (q * jnp.exp(b)).astype(BF16)
    o = lax.dot_general(qs, st.astype(BF16), (((1,), (1,)), ((), ())), preferred_element_type=F32)

    zeros_blk = jnp.zeros((BLOCK, LANES), F32)
    if reverse:
        targets = list(range(NBLK - 1))
        refs = {i: b[BLOCK * (i + 1):BLOCK * (i + 1) + 1] for i in targets}
    else:
        targets = list(range(1, NBLK))
        refs = {i: b[BLOCK * i - 1:BLOCK * i] for i in targets}
    r_full = jnp.concatenate(
        [jnp.broadcast_to(refs[i], (BLOCK, LANES)) if i in refs else zeros_blk for i in range(NBLK)], axis=0)
    qt = q * jnp.exp(jnp.minimum(b - r_full, 0.0))
    blk = lax.broadcasted_iota(jnp.int32, (CHUNK, LANES), 0) // BLOCK
    q_parts, k_parts = [], []
    for i in targets:
        q_parts.append(jnp.where(blk == i, qt, 0.0).astype(BF16))
        if reverse:
            lo, hi = BLOCK * (i + 1), CHUNK
        else:
            lo, hi = 0, BLOCK * i
        ks = k[lo:hi] * jnp.exp(refs[i] - b[lo:hi])
        pieces = []
        if lo > 0:
            pieces.append(jnp.zeros((lo, LANES), F32))
        pieces.append(ks)
        if hi < CHUNK:
            pieces.append(jnp.zeros((CHUNK - hi, LANES), F32))
        k_parts.append(jnp.concatenate(pieces, axis=0).astype(BF16))
    q_aug = jnp.concatenate(q_parts, axis=1)
    k_aug = jnp.concatenate(k_parts, axis=1)
    scores = lax.dot_general(q_aug, k_aug, (((1,), (1,)), ((), ())), preferred_element_type=F32)
    o = o + jnp.dot(scores.astype(BF16), v.astype(BF16), preferred_element_type=F32)

    q3 = q.reshape(NBLK, BLOCK, LANES)
    k3 = k.reshape(NBLK, BLOCK, LANES)
    b3 = b.reshape(NBLK, BLOCK, LANES)
    v3 = v.reshape(NBLK, BLOCK, LANES)
    pos = lax.broadcasted_iota(jnp.int32, (NBLK, BLOCK, LANES), 1)
    o3 = jnp.zeros((NBLK, BLOCK, LANES), F32)
    for p in range(BLOCK):
        kb = jnp.broadcast_to(k3[:, p:p + 1, :], (NBLK, BLOCK, LANES))
        bb = jnp.broadcast_to(b3[:, p:p + 1, :], (NBLK, BLOCK, LANES))
        vb = jnp.broadcast_to(v3[:, p:p + 1, :], (NBLK, BLOCK, LANES))
        valid = (pos <= p) if reverse else (pos >= p)
        e = jnp.exp(jnp.where(valid, b3 - bb, -1e30))
        s = jnp.sum(q3 * kb * e, axis=-1, keepdims=True)
        o3 = o3 + s * vb
    o = o + o3.reshape(CHUNK, LANES)

    kt = (k * jnp.exp(b_edge - b)).astype(BF16)
    st_new = st * jnp.exp(b_edge) + lax.dot_general(
        v.astype(BF16), kt, (((0,), (0,)), ((), ())), preferred_element_type=F32)
    return o, st_new


def _run_recurrence(load, gain, out_ref, of_scr, st_scr, n_chunks, n_ctx_chunks):
    def fwd_body(c, carry):
        r0 = pl.multiple_of(c * CHUNK, CHUNK)
        q, k, g, v, _ = load(r0, False)
        o, st = _chunk_scan(q, k, g, v, st_scr[...], False)
        st_scr[...] = st
        of_scr[pl.ds(r0, CHUNK), :] = o
        return carry

    def bwd_chunk(c):
        r0 = pl.multiple_of(c * CHUNK, CHUNK)
        q, k, g, v, gate = load(r0, True)
        o, st = _chunk_scan(q, k, g, v, st_scr[...], True)
        st_scr[...] = st
        o = o + of_scr[pl.ds(r0, CHUNK), :]
        y = o * lax.rsqrt(jnp.mean(o * o, axis=-1, keepdims=True) + EPS) * gain
        out_ref[0, pl.ds(r0, CHUNK), :] = (y * _silu(gate)).astype(out_ref.dtype)

    def bwd_ctx(i, carry):
        bwd_chunk(n_ctx_chunks - 1 - i)
        return carry

    def bwd_lat(i, carry):
        bwd_chunk(n_chunks - 1 - i)
        return carry

    st_scr[...] = jnp.zeros_like(st_scr)
    lax.fori_loop(0, n_chunks, fwd_body, 0)
    st_scr[...] = jnp.zeros_like(st_scr)
    lax.fori_loop(0, n_ctx_chunks, bwd_ctx, 0)
    lax.fori_loop(0, n_chunks - n_ctx_chunks, bwd_lat, 0)


def _hgrn2_kernel(q_ref, ff_ref, fb_ref, i_ref, g_ref, lb_ref, gain_ref, out_ref, of_scr, st_scr,
                  *, n_chunks, n_ctx_chunks):
    lbv = lb_ref[...]
    lb, log_lb, log_ub = lbv[0:1], lbv[1:2], lbv[2:3]

    def load(r0, reverse):
        rows = pl.ds(r0, CHUNK)
        z = (fb_ref if reverse else ff_ref)[0, rows, :]
        a = log_lb
        c2 = log_ub + _log_sigmoid(z)
        log_f = jnp.maximum(a, c2) + jnp.log1p(jnp.exp(-jnp.abs(a - c2)))
        k = (1.0 - lb) * _sigmoid(-z)
        return q_ref[0, rows, :], k, log_f, i_ref[0, rows, :], g_ref[0, rows, :]

    _run_recurrence(load, gain_ref[...], out_ref, of_scr, st_scr, n_chunks, n_ctx_chunks)


def _gla_kernel(q_ref, k_ref, v_ref, d_ref, r_ref, up_ref, ub_ref, gain_ref, out_ref, of_scr, st_scr,
                *, n_chunks, n_ctx_chunks):
    h = pl.program_id(1)
    lane = lax.broadcasted_iota(jnp.int32, (1, LANES), 1)
    head_mask = jnp.where((lane // GLA_DK) == (h % 2), GLA_DK ** -0.5, 0.0).astype(F32)

    def load(r0, reverse):
        rows = pl.ds(r0, CHUNK)
        direction = 1 if reverse else 0
        logit = jnp.dot(d_ref[0, rows, :], up_ref[direction, 0], preferred_element_type=F32,
                        precision=HIGHEST) + ub_ref[direction, 0]
        g = _log_sigmoid(logit) * (1.0 / GLA_GATE_NORM)
        q = q_ref[0, rows, :] * head_mask
        return q, k_ref[0, rows, :], g, v_ref[0, rows, :], r_ref[0, rows, :]

    _run_recurrence(load, gain_ref[...], out_ref, of_scr, st_scr, n_chunks, n_ctx_chunks)


def _seq_spec(t_all, col_block):
    def index_map(b, h):
        return (b, 0, col_block + h)

    return pl.BlockSpec((1, t_all, LANES), index_map)


def _hgrn2_branch(p, lb_rows, norm_g, n_ctx):
    bsz, t_all, _ = p.shape
    n_chunks, n_ctx_chunks = t_all // CHUNK, n_ctx // CHUNK
    return pl.pallas_call(
        functools.partial(_hgrn2_kernel, n_chunks=n_chunks, n_ctx_chunks=n_ctx_chunks),
        out_shape=jax.ShapeDtypeStruct((bsz, t_all, HG_KW), BF16),
        grid=(bsz, HG_HEADS),
        in_specs=[
            _seq_spec(t_all, P_HG_Q // LANES),
            _seq_spec(t_all, P_HG_FF // LANES),
            _seq_spec(t_all, P_HG_FB // LANES),
            _seq_spec(t_all, P_HG_I // LANES),
            _seq_spec(t_all, P_HG_G // LANES),
            pl.BlockSpec((SUBLANES, LANES), lambda b, h: (0, h)),
            pl.BlockSpec((1, LANES), lambda b, h: (0, h)),
        ],
        out_specs=pl.BlockSpec((1, t_all, LANES), lambda b, h: (b, 0, h)),
        scratch_shapes=[pltpu.VMEM((t_all, LANES), F32), pltpu.VMEM((LANES, LANES), F32)],
        compiler_params=pltpu.CompilerParams(
            dimension_semantics=("parallel", "parallel"), vmem_limit_bytes=VMEM_LIMIT),
        name="hgrn2_recurrence",
    )(p, p, p, p, p, lb_rows, norm_g.reshape(1, HG_KW))


def _gla_branch(p, up_pad, ub, norm_g, n_ctx):
    bsz, t_all, _ = p.shape
    n_chunks, n_ctx_chunks = t_all // CHUNK, n_ctx // CHUNK
    return pl.pallas_call(
        functools.partial(_gla_kernel, n_chunks=n_chunks, n_ctx_chunks=n_ctx_chunks),
        out_shape=jax.ShapeDtypeStruct((bsz, t_all, GLA_HEADS * LANES), BF16),
        grid=(bsz, GLA_HEADS),
        in_specs=[
            pl.BlockSpec((1, t_all, LANES), lambda b, h: (b, 0, P_GL_Q // LANES + h // 2)),
            pl.BlockSpec((1, t_all, LANES), lambda b, h: (b, 0, P_GL_K // LANES + h // 2)),
            _seq_spec(t_all, P_GL_V // LANES),
            pl.BlockSpec((1, t_all, LANES), lambda b, h: (b, 0, P_GL_D // LANES)),
            _seq_spec(t_all, P_GL_R // LANES),
            pl.BlockSpec((2, 1, LANES, LANES), lambda b, h: (0, h // 2, 0, 0)),
            pl.BlockSpec((2, 1, 1, LANES), lambda b, h: (0, h // 2, 0, 0)),
            pl.BlockSpec((1, LANES), lambda b, h: (0, h)),
        ],
        out_specs=pl.BlockSpec((1, t_all, LANES), lambda b, h: (b, 0, h)),
        scratch_shapes=[pltpu.VMEM((t_all, LANES), F32), pltpu.VMEM((LANES, LANES), F32)],
        compiler_params=pltpu.CompilerParams(
            dimension_semantics=("parallel", "parallel"), vmem_limit_bytes=VMEM_LIMIT),
        name="gla_recurrence",
    )(p, p, p, p, p, up_pad, ub, norm_g.reshape(1, GLA_HEADS * LANES))


def _s5_kernel(u_ref, bm_ref, m1_ref, cm_ref, lam_ref, o_ref, z_scr, s_scr, *, n_pairs, n_ctx_pairs, bsz):
    u = u_ref[0].astype(BF16)
    z_scr[...] = jnp.dot(u, bm_ref[0], preferred_element_type=F32)
    lam = lam_ref[0]
    half = lax.broadcasted_iota(jnp.int32, (SUBLANES, LANES), 0) < bsz

    def cmul_add(ar, ai, sr, si, zr, zi):
        return ar * sr - ai * si + zr, ar * si + ai * sr + zi

    def swap(x):
        return pltpu.roll(x, bsz, 0)

    ar, ai = lam[0:1], lam[1:2]

    def fwd_step(m, carry):
        cr, ci = carry
        rows = pl.ds(pl.multiple_of(m * SUBLANES, SUBLANES), SUBLANES)
        zr, zi = z_scr[rows, 0:LANES], z_scr[rows, LANES:2 * LANES]
        t1r, t1i = cmul_add(ar, ai, cr, ci, zr, zi)
        rr, ri = swap(t1r), swap(t1i)
        s_scr[rows, 0:LANES] = jnp.where(half, cr, rr)
        s_scr[rows, LANES:2 * LANES] = jnp.where(half, ci, ri)
        t2r, t2i = cmul_add(ar, ai, rr, ri, zr, zi)
        return swap(t2r), swap(t2i)

    zero = jnp.zeros((SUBLANES, LANES), F32)
    lax.fori_loop(0, n_pairs, fwd_step, (zero, zero))

    br, bi = lam[2:3], lam[3:4]

    def bwd_step(m, carry):
        cr, ci = carry
        rows = pl.ds(pl.multiple_of(m * SUBLANES, SUBLANES), SUBLANES)
        zr, zi = z_scr[rows, 2 * LANES:3 * LANES], z_scr[rows, 3 * LANES:4 * LANES]
        t1r, t1i = cmul_add(br, bi, cr, ci, zr, zi)
        rr, ri = swap(t1r), swap(t1i)
        s_scr[rows, 2 * LANES:3 * LANES] = jnp.where(half, rr, cr)
        s_scr[rows, 3 * LANES:4 * LANES] = jnp.where(half, ri, ci)
        t2r, t2i = cmul_add(br, bi, rr, ri, zr, zi)
        return swap(t2r), swap(t2i)

    carry = lax.fori_loop(0, n_ctx_pairs, lambda i, c: bwd_step(n_ctx_pairs - 1 - i, c), (zero, zero))
    lax.fori_loop(0, n_pairs - n_ctx_pairs, lambda i, c: bwd_step(n_pairs - 1 - i, c), carry)

    o_ref[0] = (jnp.dot(u, m1_ref[0], preferred_element_type=F32)
                + jnp.dot(s_scr[...].astype(BF16), cm_ref[0], preferred_element_type=F32))


def _s5_matrices(a_re, a_im, log_step, b_re, b_im, c_re, c_im, d_skip):
    g, n, pdim, w = S5_GROUPS, S5_STATE, S5_GROUP, BLOCK
    parts_b, parts_c, toep, lam16 = [], [], [], []
    jj = jnp.arange(w)
    for direction in range(2):
        lam_re = a_re[direction].astype(F32)
        lam_im = a_im[direction].astype(F32)
        step = jnp.exp(log_step[direction].astype(F32))[:, None]
        mag = jnp.exp(lam_re * step)
        abar_re = mag * jnp.cos(lam_im * step)
        abar_im = mag * jnp.sin(lam_im * step)
        den = lam_re * lam_re + lam_im * lam_im
        z_re = ((abar_re - 1.0) * lam_re + abar_im * lam_im) / den
        z_im = (abar_im * lam_re - (abar_re - 1.0) * lam_im) / den
        br = b_re[direction].astype(F32)
        bi = b_im[direction].astype(F32)
        bbar = (z_re[..., None] * br - z_im[..., None] * bi) + 1j * (z_re[..., None] * bi + z_im[..., None] * br)
        cc = c_re[direction].astype(F32) + 1j * c_im[direction].astype(F32)
        m = jnp.arange(w + 1, dtype=F32)[:, None, None]
        pw = jnp.exp(m * (lam_re * step)[None]) * jnp.exp(1j * (m * (lam_im * step)[None]))
        kern = jnp.real(jnp.einsum('gqn,tgn,gnp->tgpq', cc, pw[:w], bbar))
        tau = (jj[None, :] - jj[:, None]) if direction == 0 else (jj[:, None] - jj[None, :])
        tmat = jnp.where((tau >= 0)[:, :, None, None, None], kern[jnp.clip(tau, 0, w - 1)], 0.0)
        toep.append(tmat.transpose(2, 0, 3, 1, 4).reshape(g, w * pdim, w * pdim))
        idx = (w - 1 - jj) if direction == 0 else jj
        bmat = pw[idx][:, :, :, None] * bbar[None]
        bmat = bmat.transpose(1, 0, 3, 2).reshape(g, w * pdim, n)
        parts_b += [jnp.real(bmat), jnp.imag(bmat)]
        idx = (jj + 1) if direction == 0 else (w - jj)
        cmat = cc[None] * pw[idx][:, :, None, :]
        cmat = cmat.transpose(1, 3, 0, 2).reshape(g, n, w * pdim)
        parts_c += [jnp.real(cmat), -jnp.imag(cmat)]
        lam16 += [jnp.real(pw[w]), jnp.imag(pw[w])]
    eye = jnp.eye(w * pdim, dtype=F32)
    m1 = toep[0] + toep[1] + eye[None] * jnp.tile(d_skip.astype(F32).reshape(g, 1, pdim), (1, w, 1)).reshape(g, 1, w * pdim)
    gp = g // 2
    pair = jnp.eye(2, dtype=F32)
    b4 = jnp.stack(parts_b, axis=1).reshape(gp, 2, 4, w * pdim, n)
    bm = jnp.einsum('qgkrn,gh->qgrkhn', b4, pair).reshape(gp, 2 * w * pdim, 8 * n)
    c4 = jnp.stack(parts_c, axis=1).reshape(gp, 2, 4, n, w * pdim)
    cm = jnp.einsum('qgknc,gh->qkgnhc', c4, pair).reshape(gp, 8 * n, 2 * w * pdim)
    m1p = jnp.einsum('qgrc,gh->qgrhc', m1.reshape(gp, 2, w * pdim, w * pdim), pair).reshape(gp, 2 * w * pdim, 2 * w * pdim)
    lam = jnp.stack([x.reshape(gp, 2 * n) for x in lam16] + [jnp.zeros((gp, 2 * n), F32)] * 4, axis=1)
    return bm.astype(BF16), m1p.astype(BF16), cm.astype(BF16), lam


def _s5_branch(p, mats, n_ctx):
    bsz, t_all, _ = p.shape
    assert SUBLANES == 2 * bsz
    bm, m1, cm, lam = mats
    gp = S5_GROUPS // 2
    nbb = t_all // BLOCK
    nb = nbb * bsz
    assert nbb % 2 == 0 and (n_ctx // BLOCK) % 2 == 0
    u = p[:, :, P_S5_U:P_S5_U + S5_WIDTH].reshape(bsz, nbb, BLOCK, gp, 2, S5_GROUP)
    u = u.transpose(3, 1, 0, 4, 2, 5).reshape(gp, nb, 2 * BLOCK * S5_GROUP)
    wdt = 2 * BLOCK * S5_GROUP
    y = pl.pallas_call(
        functools.partial(_s5_kernel, n_pairs=nbb // 2, n_ctx_pairs=n_ctx // BLOCK // 2, bsz=bsz),
        out_shape=jax.ShapeDtypeStruct((gp, nb, wdt), F32),
        grid=(gp,),
        in_specs=[
            pl.BlockSpec((1, nb, wdt), lambda q: (q, 0, 0)),
            pl.BlockSpec((1, wdt, wdt), lambda q: (q, 0, 0)),
            pl.BlockSpec((1, wdt, wdt), lambda q: (q, 0, 0)),
            pl.BlockSpec((1, wdt, wdt), lambda q: (q, 0, 0)),
            pl.BlockSpec((1, SUBLANES, LANES), lambda q: (q, 0, 0)),
        ],
        out_specs=pl.BlockSpec((1, nb, wdt), lambda q: (q, 0, 0)),
        scratch_shapes=[pltpu.VMEM((nb, wdt), F32), pltpu.VMEM((nb, wdt), F32)],
        compiler_params=pltpu.CompilerParams(
            dimension_semantics=("parallel",), vmem_limit_bytes=VMEM_LIMIT),
        name="s5_block_scan",
    )(u, bm, m1, cm, lam)
    y = y.reshape(gp, nbb, bsz, 2, BLOCK, S5_GROUP).transpose(2, 1, 4, 0, 3, 5)
    return y.reshape(bsz, t_all, S5_WIDTH)


def _merge_kernel(ya_ref, yb_ref, yc_ref, ga_ref, gb_ref, gc_ref, x_ref, mod_ref, bp_ref, glu_ref, wo_ref,
                  o_ref, *, n_ctx, tm, row_start):
    i = pl.program_id(1)
    u = _gelu_tanh(yc_ref[0])
    u = u * _sigmoid(jnp.dot(u.astype(BF16), glu_ref[...], preferred_element_type=F32))
    merged = _sigmoid(ga_ref[0]) * jnp.dot(ya_ref[0], bp_ref[0], preferred_element_type=F32)
    merged = merged + _sigmoid(gb_ref[0]) * jnp.dot(yb_ref[0], bp_ref[1], preferred_element_type=F32)
    merged = merged + _sigmoid(gc_ref[0]) * jnp.dot(u.astype(BF16), bp_ref[2], preferred_element_type=F32)
    y = jnp.dot(merged.astype(BF16), wo_ref[...], preferred_element_type=F32)
    gate = _res_gate(mod_ref[0], tm, row_start + i * tm, n_ctx)
    o_ref[0] = x_ref[0] + gate * y


def _merge(ya, yb, yc, p, xc, mod, bp, glu, wo, n_ctx, latent_only):
    bsz, t_all, d = xc.shape
    if latent_only:
        tm = math.gcd(n_ctx, t_all - n_ctx)
        tm = _row_tile(tm, 512)
        off, n_rows = n_ctx // tm, t_all - n_ctx
    else:
        tm = _row_tile(t_all, 544)
        off, n_rows = 0, t_all
    wb = HG_KW

    def rows(width, cb=0):
        def index_map(b, i):
            return (b, i + off, cb)

        return pl.BlockSpec((1, tm, width), index_map)

    def const(shape):
        return pl.BlockSpec(shape, lambda b, i: (0,) * len(shape))

    return pl.pallas_call(
        functools.partial(_merge_kernel, n_ctx=n_ctx, tm=tm, row_start=off * tm),
        out_shape=jax.ShapeDtypeStruct((bsz, n_rows, d), F32),
        grid=(bsz, n_rows // tm),
        in_specs=[
            rows(wb), rows(wb), rows(wb),
            rows(d, P_GATES // d), rows(d, P_GATES // d + 1), rows(d, P_GATES // d + 2),
            rows(d),
            pl.BlockSpec((1, 6, d), lambda b, i: (b, 0, 0)),
            const((3, wb, d)), const((wb, wb)), const((d, d)),
        ],
        out_specs=pl.BlockSpec((1, tm, d), lambda b, i: (b, i, 0)),
        compiler_params=pltpu.CompilerParams(
            dimension_semantics=("parallel", "parallel"), vmem_limit_bytes=VMEM_LIMIT),
        name="merge_out_projection",
    )(ya, yb, yc, p, p, p, xc, mod, bp, glu, wo)


def _ffn_kernel(*refs, n_ctx, tm, routed, final):
    refs = list(refs)
    x_ref, ng_ref, mod_ref = refs[:3]
    refs = refs[3:]
    rw_ref = refs.pop(0) if routed else None
    w1_ref, w3_ref, w2_ref = refs[:3]
    refs = refs[3:]
    fg_ref = refs.pop(0) if final else None
    o_ref, h_scr, acc_scr = refs[:3]
    comb_scr = refs[3] if routed else None

    i, e, f = pl.program_id(1), pl.program_id(2), pl.program_id(3)
    first = jnp.logical_and(e == 0, f == 0)
    last = jnp.logical_and(e == pl.num_programs(2) - 1, f == pl.num_programs(3) - 1)

    @pl.when(first)
    def _():
        h = _norm_mod(x_ref[0], ng_ref[...], mod_ref[0], i * tm, n_ctx)
        h_scr[...] = h.astype(BF16)
        acc_scr[...] = jnp.zeros_like(acc_scr)
        if routed:
            logits = jnp.dot(h, rw_ref[...], preferred_element_type=F32, precision=HIGHEST)
            lane = lax.broadcasted_iota(jnp.int32, logits.shape, 1).astype(F32)
            logits = jnp.where(lane < N_EXPERTS, logits, -1e30)
            ex = jnp.exp(logits - jnp.max(logits, axis=-1, keepdims=True))
            prob = ex / jnp.sum(ex, axis=-1, keepdims=True)
            m1 = jnp.max(prob, axis=-1, keepdims=True)
            i1 = jnp.min(jnp.where(prob == m1, lane, float(LANES)), axis=-1, keepdims=True)
            rest = jnp.where(lane == i1, -1.0, prob)
            m2 = jnp.max(rest, axis=-1, keepdims=True)
            i2 = jnp.min(jnp.where(rest == m2, lane, float(LANES)), axis=-1, keepdims=True)
            comb_scr[...] = (jnp.where(lane == i1, m1, 0.0) + jnp.where(lane == i2, m2, 0.0)) / (m1 + m2)

    hb = h_scr[...]
    a = jnp.dot(hb, w1_ref[0], preferred_element_type=F32)
    b = jnp.dot(hb, w3_ref[0], preferred_element_type=F32)
    y = jnp.dot((_silu(a) * b).astype(BF16), w2_ref[0], preferred_element_type=F32)
    if routed:
        comb = comb_scr[...]
        lane = lax.broadcasted_iota(jnp.int32, comb.shape, 1)
        y = y * jnp.sum(jnp.where(lane == e, comb, 0.0), axis=-1, keepdims=True)
    acc_scr[...] += y

    @pl.when(last)
    def _():
        out = x_ref[0] + _res_gate(mod_ref[0], tm, i * tm, n_ctx) * acc_scr[...]
        if final:
            out = out * lax.rsqrt(jnp.mean(out * out, axis=-1, keepdims=True) + EPS) * fg_ref[...]
        o_ref[0] = out


def _ffn(xc, gain, mod, w1, w3, w2, n_ctx, router=None, final_gain=None):
    bsz, t_all, d = xc.shape
    n_e, _, ffn_dim = w1.shape
    routed, final = router is not None, final_gain is not None
    tm = _row_tile(t_all, 1088)
    tf = 512
    args = [xc, gain.reshape(1, d), mod]
    in_specs = [
        pl.BlockSpec((1, tm, d), lambda b, i, e, f: (b, i, 0)),
        pl.BlockSpec((1, d), lambda b, i, e, f: (0, 0)),
        pl.BlockSpec((1, 6, d), lambda b, i, e, f: (b, 0, 0)),
    ]
    if routed:
        args.append(router)
        in_specs.append(pl.BlockSpec((d, LANES), lambda b, i, e, f: (0, 0)))
    args += [w1, w3, w2]
    in_specs += [
        pl.BlockSpec((1, d, tf), lambda b, i, e, f: (e, 0, f)),
        pl.BlockSpec((1, d, tf), lambda b, i, e, f: (e, 0, f)),
        pl.BlockSpec((1, tf, d), lambda b, i, e, f: (e, f, 0)),
    ]
    if final:
        args.append(final_gain.reshape(1, d))
        in_specs.append(pl.BlockSpec((1, d), lambda b, i, e, f: (0, 0)))
    scratch = [pltpu.VMEM((tm, d), BF16), pltpu.VMEM((tm, d), F32)]
    if routed:
        scratch.append(pltpu.VMEM((tm, LANES), F32))
    return pl.pallas_call(
        functools.partial(_ffn_kernel, n_ctx=n_ctx, tm=tm, routed=routed, final=final),
        out_shape=jax.ShapeDtypeStruct((bsz, t_all, d), F32),
        grid=(bsz, t_all // tm, n_e, ffn_dim // tf),
        in_specs=in_specs,
        out_specs=pl.BlockSpec((1, tm, d), lambda b, i, e, f: (b, i, 0)),
        scratch_shapes=scratch,
        compiler_params=pltpu.CompilerParams(
            dimension_semantics=("parallel", "parallel", "arbitrary", "arbitrary"),
            vmem_limit_bytes=VMEM_LIMIT),
        name="moe_ffn" if routed else "dense_ffn",
    )(*args)


def _grid_pos_embed(rows, dim):
    rr, cc = jnp.meshgrid(jnp.arange(rows, dtype=F32), jnp.arange(GRID_W, dtype=F32), indexing='ij')
    quarter = dim // 4
    omega = 1.0 / (10000.0 ** (jnp.arange(quarter, dtype=F32) / quarter))

    def emb(pp):
        ang = pp.reshape(-1)[:, None] * omega[None, :]
        return jnp.concatenate([jnp.sin(ang), jnp.cos(ang)], axis=-1)

    return jnp.concatenate([emb(rr), emb(cc)], axis=-1)


def _reorder_w_in(w):
    d = w.shape[0]
    return jnp.concatenate(
        [w[:, :3584], w[:, 3616:4128], w[:, 4640:7712], w[:, 4128:4640], w[:, 3584:3616],
         jnp.zeros((d, P_WIDTH - 7712), w.dtype)], axis=1).astype(BF16)


def kernel(x, c, ctx, c_ctx, norm1_g, norm2_g, ada_w, ada_b, w_in, hg_lb_logits, hg_norm_g, gla_gate_up, gla_gate_b, gla_norm_g, s5_a_re, s5_a_im, s5_log_step, s5_b_re, s5_b_im, s5_c_re, s5_c_im, s5_d, s5_glu_w, branch_proj, w_out, ffn_w1, ffn_w3, ffn_w2, router_w, moe_w1, moe_w3, moe_w2, final_norm_g):
    bsz, n_tok, d = x.shape
    n_ctx = ctx.shape[1]
    depth = w_in.shape[0]
    assert bsz + 1 <= SUBLANES and n_ctx % CHUNK == 0 and n_tok % CHUNK == 0

    xc = jnp.concatenate([ctx, x + _grid_pos_embed(n_tok // GRID_W, d)[None].astype(x.dtype)], axis=1)

    cvec = jnp.concatenate([c, c_ctx[None], jnp.zeros((SUBLANES - bsz - 1, d), F32)], axis=0)
    mod_all = _modulation(cvec, ada_w, ada_b)

    lb_cum = jnp.cumsum(jax.nn.softmax(hg_lb_logits.astype(F32), axis=0), axis=0)
    lower_bounds = lb_cum - lb_cum[0:1]

    for layer in range(depth):
        last = layer == depth - 1
        m = mod_all[layer].reshape(SUBLANES, 6, d)
        lat, cx = m[:bsz], jnp.broadcast_to(m[bsz:bsz + 1], (bsz, 6, d))
        mod1 = jnp.concatenate([lat[:, 0:3], cx[:, 0:3]], axis=1)
        mod2 = jnp.concatenate([lat[:, 3:6], cx[:, 3:6]], axis=1)

        p = _in_projection(xc, norm1_g[layer], mod1, _reorder_w_in(w_in[layer]), n_ctx)

        lb = lower_bounds[layer]
        lb_rows = jnp.concatenate(
            [jnp.stack([lb, jnp.log(lb), jnp.log1p(-lb)], axis=0), jnp.zeros((SUBLANES - 3, HG_KW), F32)], axis=0)
        ya = _hgrn2_branch(p, lb_rows, hg_norm_g[layer], n_ctx)

        up = gla_gate_up[layer].astype(F32).reshape(2, GLA_RANK, 2, LANES).transpose(0, 2, 1, 3)
        up_pad = jnp.zeros((2, 2, LANES, LANES), F32)
        up_pad = up_pad.at[0, :, 0:GLA_RANK].set(up[0]).at[1, :, GLA_RANK:2 * GLA_RANK].set(up[1])
        ub = gla_gate_b[layer].astype(F32).reshape(2, 2, 1, LANES)
        yb = _gla_branch(p, up_pad, ub, gla_norm_g[layer], n_ctx)

        mats = _s5_matrices(s5_a_re[layer], s5_a_im[layer], s5_log_step[layer], s5_b_re[layer], s5_b_im[layer],
                            s5_c_re[layer], s5_c_im[layer], s5_d[layer])
        yc = _s5_branch(p, mats, n_ctx)

        xc = _merge(ya, yb, yc, p, xc, mod1, branch_proj[layer].astype(BF16), s5_glu_w[layer].astype(BF16),
                    w_out[layer].astype(BF16), n_ctx, latent_only=last)
        ffn_ctx = 0 if last else n_ctx
        if last:
            mod2 = jnp.concatenate([lat[:, 3:6], lat[:, 3:6]], axis=1)
        fin = final_norm_g if last else None
        if layer % 2 == 0:
            k = layer // 2
            xc = _ffn(xc, norm2_g[layer], mod2, ffn_w1[k][None].astype(BF16), ffn_w3[k][None].astype(BF16),
                      ffn_w2[k][None].astype(BF16), ffn_ctx, final_gain=fin)
        else:
            k = layer // 2
            rw = jnp.concatenate([router_w[k].astype(F32), jnp.zeros((d, LANES - N_EXPERTS), F32)], axis=1)
            xc = _ffn(xc, norm2_g[layer], mod2, moe_w1[k].astype(BF16), moe_w3[k].astype(BF16),
                      moe_w2[k].astype(BF16), ffn_ctx, router=rw, final_gain=fin)
        if last and xc.shape[1] != n_tok:
            xc = xc[:, n_ctx:]
    return xc
```
